```python
import math
import jax
import jax.numpy as jnp
from jax import lax
import numpy as np

D_MODEL = 1024
BATCH = 16
SEQ = 4096
DEPTH = 1

CHUNK = 64
Q_BLOCK = 128
EPS = 1e-6
NEG_INF = -1e30

GLA_HEADS = 4
GLA_HEAD_K = 64
GLA_HEAD_V = 128
GLA_K = GLA_HEADS * GLA_HEAD_K
GLA_V = GLA_HEADS * GLA_HEAD_V
GLA_LOW_RANK = 16
GLA_TAU = 16.0

DIFF_HEADS = 4
DIFF_HEAD_QK = 64
DIFF_HEAD_V = 2 * DIFF_HEAD_QK
DIFF_QK = DIFF_HEADS * 2 * DIFF_HEAD_QK
DIFF_V = DIFF_HEADS * DIFF_HEAD_V

FFN_HIDDEN = ((-(-8 * D_MODEL // 3) + 255) // 256) * 256
N_MOD = 6

IN_SIZES = (GLA_K, GLA_K, GLA_V, GLA_V, GLA_LOW_RANK, DIFF_QK, DIFF_QK, DIFF_V, D_MODEL, D_MODEL)
IN_COLS = sum(IN_SIZES)

kernel_name = "hybrid_gla_diffattn_adaln_block"


def _rmsnorm(x, g):
    xf = x.astype(jnp.float32)
    y = xf * lax.rsqrt(jnp.mean(xf * xf, axis=-1, keepdims=True) + EPS)
    return (y * g.astype(jnp.float32)).astype(x.dtype)


def _split_heads(t, n_heads):
    b, s, w = t.shape
    return t.reshape(b, s, n_heads, w // n_heads).transpose(0, 2, 1, 3)


def _merge_heads(t):
    b, h, s, d = t.shape
    return t.transpose(0, 2, 1, 3).reshape(b, s, h * d)


def _gla_chunked(q, k, v, log_alpha):
    b, h, s, dk = q.shape
    dv = v.shape[-1]
    n = s // CHUNK
    f32 = jnp.float32
    qf = q.astype(f32).reshape(b, h, n, CHUNK, dk)
    kf = k.astype(f32).reshape(b, h, n, CHUNK, dk)
    vf = v.astype(f32).reshape(b, h, n, CHUNK, dv)
    cum = jnp.cumsum(log_alpha.astype(f32).reshape(b, h, n, CHUNK, dk), axis=3)
    cum_last = cum[:, :, :, -1:, :]
    q_fwd = qf * jnp.exp(cum)
    k_fwd = kf * jnp.exp(-cum)
    q_bwd = qf * jnp.exp(-cum)
    k_bwd = kf * jnp.exp(cum)
    pos = jnp.arange(CHUNK)
    lower = pos[:, None] >= pos[None, :]
    a_intra = jnp.where(lower,
                        jnp.einsum('bhnld,bhnmd->bhnlm', q_fwd, k_fwd),
                        jnp.einsum('bhnld,bhnmd->bhnlm', q_bwd, k_bwd))
    o_intra = jnp.einsum('bhnlm,bhnmv->bhnlv', a_intra, vf)
    chunk_kv = jnp.einsum('bhnld,bhnlv->bhndv', kf * jnp.exp(cum_last - cum), vf)
    chunk_decay = jnp.exp(cum_last[:, :, :, 0, :])

    def step(state, inp):
        kv_c, dec_c = inp
        return dec_c[..., None] * state + kv_c, state

    _, states = lax.scan(step, jnp.zeros((b, h, dk, dv), f32),
                         (jnp.moveaxis(chunk_kv, 2, 0), jnp.moveaxis(chunk_decay, 2, 0)))
    states = jnp.moveaxis(states, 0, 2)
    o_inter = jnp.einsum('bhnld,bhndv->bhnlv', q_fwd, states)
    return (o_intra + o_inter).reshape(b, h, s, dv).astype(v.dtype)


def _diff_attention(q, k, v, lam):
    b, h, s, _, dh = q.shape
    n_blk = s // Q_BLOCK
    scale = dh ** -0.5
    key_chunk = jnp.arange(s) // CHUNK
    q_blocks = jnp.moveaxis(q.reshape(b, h, n_blk, Q_BLOCK, 2, dh), 2, 0)

    def one_block(args):
        q_blk, blk = args
        q_chunk = (blk * Q_BLOCK + jnp.arange(Q_BLOCK)) // CHUNK
        visible = key_chunk[None, :] <= q_chunk[:, None]
        scores = jnp.einsum('bhqmd,bhkmd->bhmqk', q_blk, k).astype(jnp.float32) * scale
        probs = jax.nn.softmax(jnp.where(visible, scores, NEG_INF), axis=-1)
        weights = probs[:, :, 0] - lam * probs[:, :, 1]
        return jnp.einsum('bhqk,bhkd->bhqd', weights.astype(v.dtype), v)

    out = lax.map(one_block, (q_blocks, jnp.arange(n_blk)))
    return jnp.moveaxis(out, 0, 2).reshape(b, h, s, v.shape[-1])


def setup_inputs(seed: int = 0) -> dict:
    key = jax.random.key(seed)
    ks = jax.random.split(key, 24)
    f32 = jnp.float32
    nrm = lambda k, shape, s: jax.random.normal(k, shape, f32) * s
    gain = lambda k, shape: 1.0 + 0.01 * jax.random.normal(k, shape, f32)
    return {
        "x": nrm(ks[0], (BATCH, SEQ, D_MODEL), 1.0),
        "c": nrm(ks[1], (BATCH, D_MODEL), 1.0),
        "w_ada": nrm(ks[2], (DEPTH, D_MODEL, N_MOD * D_MODEL), 0.5 * D_MODEL ** -0.5),
        "b_ada": nrm(ks[3], (DEPTH, N_MOD * D_MODEL), 0.01),
        "g_mix": gain(ks[4], (DEPTH, D_MODEL)),
        "w_in": nrm(ks[5], (DEPTH, D_MODEL, IN_COLS), D_MODEL ** -0.5),
        "w_alpha_up": nrm(ks[6], (DEPTH, GLA_LOW_RANK, GLA_K), GLA_LOW_RANK ** -0.5),
        "b_alpha": nrm(ks[7], (DEPTH, GLA_K), 0.1),
        "g_gla_out": gain(ks[8], (DEPTH, GLA_HEAD_V)),
        "g_diff_q": gain(ks[9], (DEPTH, DIFF_HEAD_QK)),
        "g_diff_k": gain(ks[10], (DEPTH, DIFF_HEAD_QK)),
        "lambda_q1": nrm(ks[11], (DEPTH, DIFF_HEAD_QK), 0.1),
        "lambda_k1": nrm(ks[12], (DEPTH, DIFF_HEAD_QK), 0.1),
        "lambda_q2": nrm(ks[13], (DEPTH, DIFF_HEAD_QK), 0.1),
        "lambda_k2": nrm(ks[14], (DEPTH, DIFF_HEAD_QK), 0.1),
        "g_diff_out": gain(ks[15], (DEPTH, DIFF_HEAD_V)),
        "w_br_gla": nrm(ks[16], (DEPTH, GLA_V, D_MODEL), GLA_V ** -0.5),
        "w_br_diff": nrm(ks[17], (DEPTH, DIFF_V, D_MODEL), DIFF_V ** -0.5),
        "w_out": nrm(ks[18], (DEPTH, D_MODEL, D_MODEL), D_MODEL ** -0.5),
        "g_ffn": gain(ks[19], (DEPTH, D_MODEL)),
        "w_ffn_in": nrm(ks[20], (DEPTH, D_MODEL, 2 * FFN_HIDDEN), D_MODEL ** -0.5),
        "w_ffn_out": nrm(ks[21], (DEPTH, FFN_HIDDEN, D_MODEL), FFN_HIDDEN ** -0.5),
    }


def reference(x, c, w_ada, b_ada, g_mix, w_in, w_alpha_up, b_alpha, g_gla_out,
              g_diff_q, g_diff_k, lambda_q1, lambda_k1, lambda_q2, lambda_k2, g_diff_out,
              w_br_gla, w_br_diff, w_out, g_ffn, w_ffn_in, w_ffn_out):
    b, s, _ = x.shape
    offsets = np.cumsum(IN_SIZES)[:-1].tolist()
    c_act = jax.nn.silu(c)
    for l in range(DEPTH):
        lambda_init = 0.8 - 0.6 * math.exp(-0.3 * l)
        mod = c_act @ w_ada[l] + b_ada[l]
        sh_m, sc_m, gt_m, sh_f, sc_f, gt_f = [m[:, None, :] for m in jnp.split(mod, N_MOD, axis=-1)]

        h = _rmsnorm(x, g_mix[l]) * (1.0 + sc_m) + sh_m
        proj = h @ w_in[l]
        gq, gk, gv, gr, ga, dq, dk, dv, pre_g_gla, pre_g_diff = jnp.split(proj, offsets, axis=-1)

        log_alpha = jax.nn.log_sigmoid((ga @ w_alpha_up[l] + b_alpha[l]).astype(jnp.float32)) / GLA_TAU
        o_gla = _gla_chunked(_split_heads(gq, GLA_HEADS) * (GLA_HEAD_K ** -0.5),
                             _split_heads(gk, GLA_HEADS),
                             _split_heads(gv, GLA_HEADS),
                             _split_heads(log_alpha, GLA_HEADS))
        o_gla = _merge_heads(_rmsnorm(o_gla, g_gla_out[l])) * jax.nn.silu(gr)
        y_gla = o_gla @ w_br_gla[l]

        q = _rmsnorm(dq.reshape(b, s, DIFF_HEADS, 2, DIFF_HEAD_QK), g_diff_q[l]).transpose(0, 2, 1, 3, 4)
        k = _rmsnorm(dk.reshape(b, s, DIFF_HEADS, 2, DIFF_HEAD_QK), g_diff_k[l]).transpose(0, 2, 1, 3, 4)
        v = _split_heads(dv, DIFF_HEADS)
        lam = (jnp.exp(jnp.sum(lambda_q1[l].astype(jnp.float32) * lambda_k1[l].astype(jnp.float32)))
               - jnp.exp(jnp.sum(lambda_q2[l].astype(jnp.float32) * lambda_k2[l].astype(jnp.float32)))
               + lambda_init)
        o_diff = _diff_attention(q, k, v, lam)
        o_diff = _merge_heads(_rmsnorm(o_diff, g_diff_out[l]) * (1.0 - lambda_init))
        y_diff = o_diff @ w_br_diff[l]

        mixed = jax.nn.sigmoid(pre_g_gla) * y_gla + jax.nn.sigmoid(pre_g_diff) * y_diff
        x = x + gt_m * (mixed @ w_out[l])

        h2 = _rmsnorm(x, g_ffn[l]) * (1.0 + sc_f) + sh_f
        gate, up = jnp.split(h2 @ w_ffn_in[l], 2, axis=-1)
        x = x + gt_f * ((jax.nn.silu(gate) * up) @ w_ffn_out[l])
    return x
```

```python
import functools
import math

import jax
import jax.numpy as jnp
from jax import lax
from jax.experimental import pallas as pl
from jax.experimental.pallas import tpu as pltpu

CHUNK = 64
EPS = 1e-6
NEG_INF = -1e30
N_MOD = 6

GLA_HEADS = 4
GLA_HEAD_K = 64
GLA_HEAD_V = 128
GLA_LOW_RANK = 16
GLA_TAU = 16.0
DIFF_HEADS = 4
DIFF_HEAD_QK = 64
DIFF_HEAD_V = 128

LANES = 128
ROW_TILE = 512
GLA_BLOCK = 256
ATT_BLOCK = 256
VMEM_LIMIT = 56 * 1024 * 1024

F32 = jnp.float32
BF16 = jnp.bfloat16


def _dot(a, b):
    return jnp.dot(a, b, preferred_element_type=F32)


def _dot_nt(a, b):
    return lax.dot_general(a, b, (((1,), (1,)), ((), ())), preferred_element_type=F32)


def _dot_tn(a, b):
    return lax.dot_general(a, b, (((0,), (0,)), ((), ())), preferred_element_type=F32)


def _const_spec(shape):
    return pl.BlockSpec(shape, lambda *_: (0,) * len(shape), pipeline_mode=pl.Buffered(1))


def _mod_kernel(c_ref, w_ref, b_ref, o_ref):
    c = c_ref[...]
    c_act = c * jax.nn.sigmoid(c)
    o_ref[...] = jnp.dot(c_act, w_ref[...], preferred_element_type=F32,
                         precision=lax.Precision.HIGHEST) + b_ref[...]


def _mod_call(c, w_ada, b_ada):
    b, d = c.shape
    n = w_ada.shape[1]
    return pl.pallas_call(
        _mod_kernel,
        grid=(n // d,),
        in_specs=[pl.BlockSpec((b, d), lambda j: (0, 0)),
                  pl.BlockSpec((d, d), lambda j: (0, j)),
                  pl.BlockSpec((1, d), lambda j: (0, j))],
        out_specs=pl.BlockSpec((b, d), lambda j: (0, j)),
        out_shape=jax.ShapeDtypeStruct((b, n), F32),
        name="mod",
    )(c, w_ada, b_ada.reshape(1, n))


def _rms_mod(x, g, scale, shift):
    y = x * lax.rsqrt(jnp.mean(x * x, axis=-1, keepdims=True) + EPS)
    return (y * g) * (1.0 + scale) + shift


def _proj_kernel(offs, x_ref, mod_ref, g_ref, w_ref, wup_ref, bal_ref, grp_ref, gq_ref, gk_ref,
                 oq_ref, ok_ref, ov_ref, or_ref, ola_ref, odq_ref, odk_ref, odvt_ref, osg_ref, osd_ref):
    o_gq, o_gk, o_gv, o_gr, o_dq, o_dk, o_dv, o_pg, o_pd, o_ga, o_end = offs
    mod = mod_ref[0]
    h = _rms_mod(x_ref[0], g_ref[...], mod[1:2, :], mod[0:1, :]).astype(BF16)

    def seg(lo, hi):
        return _dot(h, w_ref[:, lo:hi])

    oq_ref[0] = (seg(o_gq, o_gk) * (GLA_HEAD_K ** -0.5)).astype(BF16)
    ok_ref[0] = seg(o_gk, o_gv).astype(BF16)
    ov_ref[0] = seg(o_gv, o_gr).astype(BF16)
    gr = seg(o_gr, o_dq)
    or_ref[0] = (gr * jax.nn.sigmoid(gr)).astype(BF16)

    ga = seg(o_ga, o_end).astype(BF16)
    z = _dot(ga, wup_ref[...]) + bal_ref[...]
    ola_ref[0] = (jnp.minimum(z, 0.0) - jnp.log(1.0 + jnp.exp(-jnp.abs(z)))) * (1.0 / GLA_TAU)

    def qk_norm(t, gain):
        ms = _dot((t * t).astype(BF16), grp_ref[...])
        return t * lax.rsqrt(ms + EPS) * gain

    odq_ref[0] = (qk_norm(seg(o_dq, o_dk), gq_ref[...]) * (DIFF_HEAD_QK ** -0.5)).astype(BF16)
    odk_ref[0] = qk_norm(seg(o_dk, o_dv), gk_ref[...]).astype(BF16)
    odvt_ref[0] = seg(o_dv, o_pg).T.astype(BF16)
    osg_ref[0] = jax.nn.sigmoid(seg(o_pg, o_pd)).astype(BF16)
    osd_ref[0] = jax.nn.sigmoid(seg(o_pd, o_ga)).astype(BF16)


def _proj_call(x, mod3, g_mix, w_in, w_alpha_up, b_alpha, g_diff_q, g_diff_k):
    b, s, d = x.shape
    gla_k = GLA_HEADS * GLA_HEAD_K
    gla_v = GLA_HEADS * GLA_HEAD_V
    diff_qk = DIFF_HEADS * 2 * DIFF_HEAD_QK
    diff_v = DIFF_HEADS * DIFF_HEAD_V
    sizes = (gla_k, gla_k, gla_v, gla_v, GLA_LOW_RANK, diff_qk, diff_qk, diff_v, d, d)
    src = [0]
    for n in sizes:
        src.append(src[-1] + n)
    w_perm = jnp.concatenate(
        [w_in[:, :src[4]], w_in[:, src[5]:], w_in[:, src[4]:src[5]],
         jnp.zeros((d, LANES - GLA_LOW_RANK), w_in.dtype)], axis=1).astype(BF16)
    perm_sizes = (gla_k, gla_k, gla_v, gla_v, diff_qk, diff_qk, diff_v, d, d, LANES)
    offs = [0]
    for n in perm_sizes:
        offs.append(offs[-1] + n)
    n_cols = offs[-1]
    w_up = jnp.concatenate([w_alpha_up, jnp.zeros((LANES - GLA_LOW_RANK, gla_k), w_alpha_up.dtype)],
                           axis=0).astype(BF16)
    grp = jnp.arange(diff_qk) // DIFF_HEAD_QK
    grp_mat = jnp.where(grp[:, None] == grp[None, :], 1.0 / DIFF_HEAD_QK, 0.0).astype(BF16)
    n_grp = diff_qk // DIFF_HEAD_QK
    tm = ROW_TILE
    tok = lambda w: pl.BlockSpec((1, tm, w), lambda bi, i: (bi, i, 0))
    tok_shape = lambda w, dt=BF16: jax.ShapeDtypeStruct((b, s, w), dt)
    return pl.pallas_call(
        functools.partial(_proj_kernel, tuple(offs)),
        grid=(b, s // tm),
        in_specs=[tok(d),
                  pl.BlockSpec((1, N_MOD, d), lambda bi, i: (bi, 0, 0)),
                  _const_spec((1, d)),
                  _const_spec((d, n_cols)),
                  _const_spec((LANES, gla_k)),
                  _const_spec((1, gla_k)),
                  _const_spec((diff_qk, diff_qk)),
                  _const_spec((1, diff_qk)),
                  _const_spec((1, diff_qk))],
        out_specs=[tok(gla_k), tok(gla_k), tok(gla_v), tok(gla_v), tok(gla_k), tok(diff_qk), tok(diff_qk),
                   pl.BlockSpec((1, diff_v, tm), lambda bi, i: (bi, 0, i)),
                   tok(d), tok(d)],
        out_shape=[tok_shape(gla_k), tok_shape(gla_k), tok_shape(gla_v), tok_shape(gla_v),
                   tok_shape(gla_k, F32), tok_shape(diff_qk), tok_shape(diff_qk),
                   jax.ShapeDtypeStruct((b, diff_v, s), BF16),
                   tok_shape(d), tok_shape(d)],
        compiler_params=pltpu.CompilerParams(dimension_semantics=("arbitrary", "arbitrary"),
                                             vmem_limit_bytes=VMEM_LIMIT),
        name="proj",
    )(x, mod3, g_mix.reshape(1, d), w_perm, w_up, b_alpha.reshape(1, gla_k), grp_mat,
      jnp.tile(g_diff_q, n_grp).reshape(1, diff_qk), jnp.tile(g_diff_k, n_grp).reshape(1, diff_qk))


def _gla_kernel(q_ref, k_ref, v_ref, r_ref, la_ref, g_ref, o_ref, state_ref):
    tb = GLA_BLOCK
    n_chunks = tb // CHUNK
    dk_all = GLA_HEADS * GLA_HEAD_K
    dv_all = GLA_HEADS * GLA_HEAD_V

    @pl.when(pl.program_id(1) == 0)
    def _():
        state_ref[...] = jnp.zeros_like(state_ref)

    row = lax.broadcasted_iota(jnp.int32, (tb, tb), 0)
    col = lax.broadcasted_iota(jnp.int32, (tb, tb), 1)
    same_chunk = (row // CHUNK) == (col // CHUNK)
    lower = jnp.logical_and(same_chunk, row >= col)
    upper = jnp.logical_and(same_chunk, row < col)

    la = la_ref[0]
    la_hi = la.astype(BF16)
    la_lo = (la - la_hi.astype(F32)).astype(BF16)
    tri = jnp.where(lower, 1.0, 0.0).astype(BF16)
    cum = _dot(tri, la_hi) + _dot(tri, la_lo)
    cum_last = jnp.concatenate(
        [jnp.broadcast_to(cum[(c + 1) * CHUNK - 1:(c + 1) * CHUNK, :], (CHUNK, dk_all)) for c in range(n_chunks)],
        axis=0)
    e_pos = jnp.exp(cum)
    e_neg = jnp.exp(-cum)
    q = q_ref[0].astype(F32)
    k = k_ref[0].astype(F32)
    q_fwd = (q * e_pos).astype(BF16)
    q_bwd = (q * e_neg).astype(BF16)
    k_fwd = (k * e_neg).astype(BF16)
    k_bwd = (k * e_pos).astype(BF16)
    k_dec = (k * jnp.exp(cum_last - cum)).astype(BF16)
    v = v_ref[0]

    srow = lax.broadcasted_iota(jnp.int32, (dv_all, dk_all), 0) // GLA_HEAD_V
    scol = lax.broadcasted_iota(jnp.int32, (dv_all, dk_all), 1) // GLA_HEAD_K
    same_head = srow == scol
    inter = []
    for c in range(n_chunks):
        rows = slice(c * CHUNK, (c + 1) * CHUNK)
        state = state_ref[...]
        inter.append(_dot_nt(q_fwd[rows], state.astype(BF16)))
        kv_t = _dot_tn(v[rows], k_dec[rows])
        decay = jnp.exp(cum[(c + 1) * CHUNK - 1:(c + 1) * CHUNK, :])
        state_ref[...] = state * decay + jnp.where(same_head, kv_t, 0.0)
    o_inter = jnp.concatenate(inter, axis=0)

    lane_head = lax.broadcasted_iota(jnp.int32, (1, dk_all), 1) // GLA_HEAD_K
    for hd in range(GLA_HEADS):
        sel = lane_head == hd
        a_fwd = _dot_nt(jnp.where(sel, q_fwd, jnp.zeros_like(q_fwd)), k_fwd)
        a_bwd = _dot_nt(jnp.where(sel, q_bwd, jnp.zeros_like(q_bwd)), k_bwd)
        a = jnp.where(lower, a_fwd, jnp.where(upper, a_bwd, 0.0)).astype(BF16)
        cols = slice(hd * GLA_HEAD_V, (hd + 1) * GLA_HEAD_V)
        o = _dot(a, v[:, cols]) + o_inter[:, cols]
        o = o * lax.rsqrt(jnp.mean(o * o, axis=-1, keepdims=True) + EPS) * g_ref[...]
        o_ref[0, :, cols] = (o * r_ref[0, :, cols].astype(F32)).astype(BF16)


def _gla_call(q, k, v, r, la, g_out):
    b, s, dk_all = q.shape
    dv_all = v.shape[-1]
    tb = GLA_BLOCK
    tok = lambda w: pl.BlockSpec((1, tb, w), lambda bi, i: (bi, i, 0))
    return pl.pallas_call(
        _gla_kernel,
        grid=(b, s // tb),
        in_specs=[tok(dk_all), tok(dk_all), tok(dv_all), tok(dv_all), tok(dk_all),
                  _const_spec((1, GLA_HEAD_V))],
        out_specs=tok(dv_all),
        out_shape=jax.ShapeDtypeStruct((b, s, dv_all), BF16),
        scratch_shapes=[pltpu.VMEM((dv_all, dk_all), F32)],
        compiler_params=pltpu.CompilerParams(dimension_semantics=("arbitrary", "arbitrary"),
                                             vmem_limit_bytes=VMEM_LIMIT),
        name="gla",
    )(q, k, v, r, la, g_out.reshape(1, GLA_HEAD_V))


def _attn_kernel(lam_init, q_ref, k_ref, vt_ref, lq1_ref, lk1_ref, lq2_ref, lk2_ref, g_ref, o_ref,
                 acc_ref, m_ref, l_ref):
    t = ATT_BLOCK
    n_blocks = q_ref.shape[1] // t
    lam = (jnp.exp(jnp.sum(lq1_ref[...] * lk1_ref[...])) - jnp.exp(jnp.sum(lq2_ref[...] * lk2_ref[...]))
           + lam_init)
    lane = lax.broadcasted_iota(jnp.int32, (1, 2 * DIFF_HEAD_QK), 1)
    first = lane < DIFF_HEAD_QK
    key_chunk = lax.broadcasted_iota(jnp.int32, (t, t), 0) // CHUNK
    qry_chunk = lax.broadcasted_iota(jnp.int32, (t, t), 1) // CHUNK
    visible = key_chunk <= qry_chunk

    def q_block(qi, carry):
        q = q_ref[0, pl.ds(pl.multiple_of(qi * t, t), t), :]
        zero = jnp.zeros_like(q)
        q_comp = (jnp.where(first, q, zero), jnp.where(first, zero, q))
        m_ref[...] = jnp.full_like(m_ref, NEG_INF)
        l_ref[...] = jnp.zeros_like(l_ref)
        acc_ref[...] = jnp.zeros_like(acc_ref)

        def kv_step(j, masked):
            start = pl.multiple_of(j * t, t)
            k = k_ref[0, pl.ds(start, t), :]
            vt = vt_ref[0, :, pl.ds(start, t)]
            for comp in range(2):
                s_t = _dot_nt(k, q_comp[comp])
                if masked:
                    s_t = jnp.where(visible, s_t, NEG_INF)
                m_old = m_ref[comp]
                m_new = jnp.maximum(m_old, jnp.max(s_t, axis=0, keepdims=True))
                alpha = jnp.exp(m_old - m_new)
                p = jnp.exp(s_t - m_new)
                l_ref[comp] = alpha * l_ref[comp] + jnp.sum(p, axis=0, keepdims=True)
                acc_ref[comp] = alpha * acc_ref[comp] + _dot(vt, p.astype(BF16))
                m_ref[comp] = m_new

        def full_step(j, c):
            kv_step(j, False)
            return c

        lax.fori_loop(0, qi, full_step, 0)
        kv_step(qi, True)

        out = acc_ref[0] / l_ref[0] - lam * (acc_ref[1] / l_ref[1])
        out = out * lax.rsqrt(jnp.mean(out * out, axis=0, keepdims=True) + EPS)
        out = out * g_ref[...] * (1.0 - lam_init)
        o_ref[0, pl.ds(pl.multiple_of(qi * t, t), t), :] = out.T.astype(BF16)
        return carry

    lax.fori_loop(0, n_blocks, q_block, 0)


def _attn_call(q, k, vt, lq1, lk1, lq2, lk2, g_out, lam_init):
    b, s, _ = q.shape
    t = ATT_BLOCK
    hw = 2 * DIFF_HEAD_QK
    vec = lambda a: a.reshape(1, DIFF_HEAD_QK)
    return pl.pallas_call(
        functools.partial(_attn_kernel, lam_init),
        grid=(b, DIFF_HEADS),
        in_specs=[pl.BlockSpec((1, s, hw), lambda bi, hi: (bi, 0, hi)),
                  pl.BlockSpec((1, s, hw), lambda bi, hi: (bi, 0, hi)),
                  pl.BlockSpec((1, DIFF_HEAD_V, s), lambda bi, hi: (bi, hi, 0)),
                  _const_spec((1, DIFF_HEAD_QK)), _const_spec((1, DIFF_HEAD_QK)),
                  _const_spec((1, DIFF_HEAD_QK)), _const_spec((1, DIFF_HEAD_QK)),
                  _const_spec((DIFF_HEAD_V, 1))],
        out_specs=pl.BlockSpec((1, s, DIFF_HEAD_V), lambda bi, hi: (bi, 0, hi)),
        out_shape=jax.ShapeDtypeStruct((b, s, DIFF_HEADS * DIFF_HEAD_V), BF16),
        scratch_shapes=[pltpu.VMEM((2, DIFF_HEAD_V, t), F32),
                        pltpu.VMEM((2, 1, t), F32),
                        pltpu.VMEM((2, 1, t), F32)],
        compiler_params=pltpu.CompilerParams(dimension_semantics=("arbitrary", "arbitrary"),
                                             vmem_limit_bytes=VMEM_LIMIT),
        name="attn",
    )(q, k, vt, vec(lq1), vec(lk1), vec(lq2), vec(lk2), g_out.reshape(DIFF_HEAD_V, 1))


def _mix_kernel(x_ref, og_ref, od_ref, sg_ref, sd_ref, mod_ref, wg_ref, wd_ref, wo_ref, o_ref):
    y_gla = _dot(og_ref[0], wg_ref[...])
    y_diff = _dot(od_ref[0], wd_ref[...])
    mixed = sg_ref[0].astype(F32) * y_gla + sd_ref[0].astype(F32) * y_diff
    upd = _dot(mixed.astype(BF16), wo_ref[...])
    o_ref[0] = x_ref[0] + mod_ref[0][2:3, :] * upd


def _mix_call(x, o_gla, o_diff, sg, sd, mod3, w_br_gla, w_br_diff, w_out):
    b, s, d = x.shape
    tm = ROW_TILE
    tok = lambda w: pl.BlockSpec((1, tm, w), lambda bi, i: (bi, i, 0))
    return pl.pallas_call(
        _mix_kernel,
        grid=(b, s // tm),
        in_specs=[tok(d), tok(o_gla.shape[-1]), tok(o_diff.shape[-1]), tok(d), tok(d),
                  pl.BlockSpec((1, N_MOD, d), lambda bi, i: (bi, 0, 0)),
                  _const_spec(w_br_gla.shape), _const_spec(w_br_diff.shape), _const_spec(w_out.shape)],
        out_specs=tok(d),
        out_shape=jax.ShapeDtypeStruct((b, s, d), F32),
        compiler_params=pltpu.CompilerParams(dimension_semantics=("arbitrary", "arbitrary"),
                                             vmem_limit_bytes=VMEM_LIMIT),
        name="mix",
    )(x, o_gla, o_diff, sg, sd, mod3, w_br_gla.astype(BF16), w_br_diff.astype(BF16), w_out.astype(BF16))


def _ffn_kernel(n_split, x_ref, mod_ref, g_ref, wi_ref, wo_ref, o_ref):
    x = x_ref[0]
    mod = mod_ref[0]
    h = _rms_mod(x, g_ref[...], mod[4:5, :], mod[3:4, :]).astype(BF16)
    hidden = wo_ref.shape[0]
    step = hidden // n_split
    acc = None
    for i in range(n_split):
        gate = _dot(h, wi_ref[:, i * step:(i + 1) * step])
        up = _dot(h, wi_ref[:, hidden + i * step:hidden + (i + 1) * step])
        act = (gate * jax.nn.sigmoid(gate) * up).astype(BF16)
        part = _dot(act, wo_ref[i * step:(i + 1) * step, :])
        acc = part if acc is None else acc + part
    o_ref[0] = x + mod[5:6, :] * acc


def _ffn_call(x, mod3, g_ffn, w_ffn_in, w_ffn_out):
    b, s, d = x.shape
    tm = ROW_TILE
    hidden = w_ffn_out.shape[0]
    n_split = 2 if hidden % (2 * LANES) == 0 else 1
    tok = pl.BlockSpec((1, tm, d), lambda bi, i: (bi, i, 0))
    return pl.pallas_call(
        functools.partial(_ffn_kernel, n_split),
        grid=(b, s // tm),
        in_specs=[tok,
                  pl.BlockSpec((1, N_MOD, d), lambda bi, i: (bi, 0, 0)),
                  _const_spec((1, d)), _const_spec(w_ffn_in.shape), _const_spec(w_ffn_out.shape)],
        out_specs=tok,
        out_shape=jax.ShapeDtypeStruct((b, s, d), F32),
        compiler_params=pltpu.CompilerParams(dimension_semantics=("arbitrary", "arbitrary"),
                                             vmem_limit_bytes=VMEM_LIMIT),
        name="ffn",
    )(x, mod3, g_ffn.reshape(1, d), w_ffn_in.astype(BF16), w_ffn_out.astype(BF16))


def kernel(x, c, w_ada, b_ada, g_mix, w_in, w_alpha_up, b_alpha, g_gla_out, g_diff_q, g_diff_k,
           lambda_q1, lambda_k1, lambda_q2, lambda_k2, g_diff_out, w_br_gla, w_br_diff, w_out,
           g_ffn, w_ffn_in, w_ffn_out):
    b, s, d = x.shape
    for l in range(w_in.shape[0]):
        lam_init = 0.8 - 0.6 * math.exp(-0.3 * l)
        mod3 = _mod_call(c, w_ada[l], b_ada[l]).reshape(b, N_MOD, d)
        gq, gk, gv, gr, la, dq, dk, dvt, sg, sd = _proj_call(
            x, mod3, g_mix[l], w_in[l], w_alpha_up[l], b_alpha[l], g_diff_q[l], g_diff_k[l])
        o_gla = _gla_call(gq, gk, gv, gr, la, g_gla_out[l])
        o_diff = _attn_call(dq, dk, dvt, lambda_q1[l], lambda_k1[l], lambda_q2[l], lambda_k2[l],
                            g_diff_out[l], lam_init)
        x = _mix_call(x, o_gla, o_diff, sg, sd, mod3, w_br_gla[l], w_br_diff[l], w_out[l])
        x = _ffn_call(x, mod3, g_ffn[l], w_ffn_in[l], w_ffn_out[l])
    return x
```
